```python
import jax
import jax.numpy as jnp
from jax import lax
import numpy as np

D_MODEL = 2048
BATCH = 16
SEQ = 2048
DEPTH = 4
DEC_BATCH = 32
DEC_SEQ = 16
PAST_LEN = 2048

CHUNK = 64
QBLOCK = 128
A_HEADS = 16
HEAD_DIM = 64
IDX_HEADS = 8
IDX_DIM = 64
TOPK_MAX = 256
ROPE_THETA = 10000.0
M_HEADS = 4
M_DK = 128
M_DV = 256
N_EXPERTS = 32
TOP_K = 4
D_EXPERT = D_MODEL // 2
MOE_BLOCK = 1024
SWIGLU_LIMIT = 7.0
SWIGLU_ALPHA = 1.702
DN_ALPHA = (2.0 * DEPTH) ** 0.25
DN_BETA = (8.0 * DEPTH) ** -0.25
LN_EPS = 1e-5

A_WIDTH = A_HEADS * HEAD_DIM
M_WIDTH = M_HEADS * M_DV
MIX_WIDTH = A_WIDTH + M_WIDTH
PROJ_SPLITS = (A_WIDTH, HEAD_DIM, HEAD_DIM, IDX_HEADS * IDX_DIM, IDX_DIM, IDX_HEADS,
               M_HEADS * M_DK, M_HEADS * M_DK, M_WIDTH, M_HEADS, M_HEADS, M_WIDTH)
PROJ_WIDTH = sum(PROJ_SPLITS)
PROJ_OFFSETS = tuple(sum(PROJ_SPLITS[:i + 1]) for i in range(len(PROJ_SPLITS) - 1))

kernel_name = 'streaming_dsa_mlstm_moe_trunk'


def _layer_norm(x, g, b):
    xf = x.astype(jnp.float32)
    mu = xf.mean(-1, keepdims=True)
    var = jnp.square(xf - mu).mean(-1, keepdims=True)
    y = (xf - mu) * lax.rsqrt(var + LN_EPS) * g.astype(jnp.float32) + b.astype(jnp.float32)
    return y.astype(x.dtype)


def _head_norm(h, g):
    hf = h.astype(jnp.float32)
    mu = hf.mean(-1, keepdims=True)
    var = jnp.square(hf - mu).mean(-1, keepdims=True)
    return (hf - mu) * lax.rsqrt(var + LN_EPS) * g.astype(jnp.float32).reshape(M_HEADS, M_DV)


def _rope(x, pos):
    d = x.shape[-1]
    inv = ROPE_THETA ** (-jnp.arange(0, d, 2, dtype=jnp.float32) / d)
    ang = pos.astype(jnp.float32)[:, None] * inv[None, :]
    ang = ang.reshape((ang.shape[0],) + (1,) * (x.ndim - 3) + (d // 2,))
    cos = jnp.cos(ang).astype(x.dtype)
    sin = jnp.sin(ang).astype(x.dtype)
    x1, x2 = jnp.split(x, 2, axis=-1)
    return jnp.concatenate([x1 * cos - x2 * sin, x2 * cos + x1 * sin], axis=-1)


def _dsa_block(q, qi, wi, q_pos, k_all, v_all, ki_all, key_pos, topk):
    dots = jnp.einsum('bthd,bsd->bths', qi.astype(jnp.float32), ki_all.astype(jnp.float32))
    scores = jnp.einsum('bth,bths->bts', wi.astype(jnp.float32), jax.nn.relu(dots))
    adm = (key_pos[None, :] // CHUNK) <= (q_pos[:, None] // CHUNK)
    scores = jnp.where(adm[None], scores, -jnp.inf)
    _, idx = lax.top_k(scores, topk)
    valid = adm[jnp.arange(q_pos.shape[0])[None, :, None], idx]
    gather = jax.vmap(lambda rows, ids: rows[ids])
    k_sel = gather(k_all, idx)
    v_sel = gather(v_all, idx)
    logits = jnp.einsum('bthd,btkd->bthk', q, k_sel).astype(jnp.float32)
    logits = jnp.where(valid[:, :, None, :], logits, -jnp.inf)
    p = jax.nn.softmax(logits, axis=-1).astype(v_sel.dtype)
    return jnp.einsum('bthk,btkd->bthd', p, v_sel)


def _dsa_prompt(q, qi, wi, pos, k, v, ki, topk):
    B, T = q.shape[:2]
    nb = T // QBLOCK

    def blocks(a):
        return a.reshape((B, nb, QBLOCK) + a.shape[2:]).swapaxes(0, 1)

    xs = (blocks(q), blocks(qi), blocks(wi), pos.reshape(nb, QBLOCK))
    out = lax.map(lambda b: _dsa_block(b[0], b[1], b[2], b[3], k, v, ki, pos, topk), xs)
    return out.swapaxes(0, 1).reshape(B, T, A_HEADS, HEAD_DIM)


def _mlstm(q, k, v, ig, fg, C0, n0, m0):
    B, T, H, _ = q.shape
    L = CHUNK if T % CHUNK == 0 else T
    nc = T // L

    def chunks(a):
        a = a.astype(jnp.float32).reshape((B, nc, L) + a.shape[2:])
        return jnp.moveaxis(a, (1, 2), (0, 3))

    logf = jax.nn.log_sigmoid(fg.astype(jnp.float32))
    tril = jnp.tril(jnp.ones((L, L), dtype=bool))

    def step(carry, xs):
        C, n, m = carry
        qc, kc, vc, ic, fc = xs
        b = jnp.cumsum(fc, axis=-1)
        a = b + m[..., None]
        Dm = jnp.where(tril, b[..., :, None] - b[..., None, :] + ic[..., None, :], -jnp.inf)
        mt = jnp.maximum(a, Dm.max(-1))
        inter = jnp.exp(a - mt)
        S = jnp.einsum('bhtd,bhsd->bhts', qc, kc) * jnp.exp(Dm - mt[..., None])
        num = jnp.einsum('bhts,bhsv->bhtv', S, vc) + inter[..., None] * jnp.einsum('bhvd,bhtd->bhtv', C, qc)
        den = S.sum(-1) + inter * jnp.einsum('bhd,bhtd->bht', n, qc)
        hc = num / jnp.maximum(jnp.abs(den), jnp.exp(-mt))[..., None]
        bL = b[..., -1]
        g = bL[..., None] - b + ic
        m_new = jnp.maximum(bL + m, g.max(-1))
        decay = jnp.exp(bL + m - m_new)
        ws = jnp.exp(g - m_new[..., None])
        C_new = decay[..., None, None] * C + jnp.einsum('bhsv,bhsd->bhvd', ws[..., None] * vc, kc)
        n_new = decay[..., None] * n + jnp.einsum('bhs,bhsd->bhd', ws, kc)
        return (C_new, n_new, m_new), hc

    init = (C0.astype(jnp.float32), n0.astype(jnp.float32), m0.astype(jnp.float32))
    (C, n, m), h = lax.scan(step, init, (chunks(q), chunks(k), chunks(v), chunks(ig), chunks(logf)))
    h = jnp.moveaxis(h, (0, 3), (1, 2)).reshape(B, T, H, v.shape[-1])
    return h, (C, n, m)


def _mixer(h, pos, w_in, b_in, w_out, mh_g, past, st0, topk):
    B, T, _ = h.shape
    proj = h @ w_in + b_in
    aq, ak, av, iq, ik, iw, mq, mk, mv, mi, mf, mo = jnp.split(proj, PROJ_OFFSETS, axis=-1)
    aq = _rope(aq.reshape(B, T, A_HEADS, HEAD_DIM), pos) * (HEAD_DIM ** -0.5)
    ak = _rope(ak, pos)
    iq = _rope(iq.reshape(B, T, IDX_HEADS, IDX_DIM), pos) * (IDX_DIM ** -0.5)
    ik = _rope(ik, pos)
    iw = iw * (IDX_HEADS ** -0.5)
    if past is None:
        attn = _dsa_prompt(aq, iq, iw, pos, ak, av, ik, topk)
    else:
        pk, pv, pki = past
        k_all = jnp.concatenate([pk, ak], axis=1)
        v_all = jnp.concatenate([pv, av], axis=1)
        ki_all = jnp.concatenate([pki, ik], axis=1)
        key_pos = jnp.arange(k_all.shape[1])
        attn = _dsa_block(aq, iq, iw, pos, k_all, v_all, ki_all, key_pos, topk)
    hm, st = _mlstm(mq.reshape(B, T, M_HEADS, M_DK) * (M_DK ** -0.5), mk.reshape(B, T, M_HEADS, M_DK),
                    mv.reshape(B, T, M_HEADS, M_DV), mi, mf, st0[0], st0[1], st0[2])
    hm = _head_norm(hm, mh_g) * jax.nn.sigmoid(mo.astype(jnp.float32)).reshape(B, T, M_HEADS, M_DV)
    mix = jnp.concatenate([attn.reshape(B, T, A_WIDTH), hm.reshape(B, T, M_WIDTH).astype(h.dtype)], axis=-1)
    return mix @ w_out, (ak, av, ik), st


def _moe(x, rw, rb, wg, bg, wu, bu, wd, bd):
    N, D = x.shape
    A = N * TOP_K
    G = int(min(MOE_BLOCK, max(8, A // N_EXPERTS)))
    n_blocks = (A + N_EXPERTS * (G - 1) + G - 1) // G
    R = n_blocks * G
    logits = (x @ rw + rb).astype(jnp.float32)
    top_vals, top_idx = lax.top_k(logits, TOP_K)
    gates = jax.nn.softmax(top_vals, axis=-1)
    e_flat = top_idx.reshape(-1)
    tok = jnp.arange(A, dtype=jnp.int32) // TOP_K
    order = jnp.argsort(e_flat, stable=True)
    e_sorted = e_flat[order]
    counts = jnp.bincount(e_flat, length=N_EXPERTS)
    padded = ((counts + G - 1) // G) * G
    pad_end = jnp.cumsum(padded)
    pad_start = pad_end - padded
    grp_start = jnp.cumsum(counts) - counts
    dest = pad_start[e_sorted] + (jnp.arange(A) - grp_start[e_sorted])
    rows_tok = jnp.zeros((R,), jnp.int32).at[dest].set(tok[order])
    rows_gate = jnp.zeros((R,), x.dtype).at[dest].set(gates.reshape(-1)[order].astype(x.dtype))
    block_expert = jnp.minimum(jnp.searchsorted(pad_end, jnp.arange(n_blocks) * G, side='right'),
                               N_EXPERTS - 1)
    xb = x[rows_tok].reshape(n_blocks, G, D)

    def expert_block(args):
        xe, e = args
        g = jnp.minimum(xe @ wg[e] + bg[e], SWIGLU_LIMIT)
        u = jnp.clip(xe @ wu[e] + bu[e], -SWIGLU_LIMIT, SWIGLU_LIMIT)
        hdn = g * jax.nn.sigmoid(SWIGLU_ALPHA * g) * (u + 1.0)
        return hdn @ wd[e] + bd[e]

    yb = lax.map(expert_block, (xb, block_expert)).reshape(R, D)
    return jnp.zeros_like(x).at[rows_tok].add(yb * rows_gate[:, None])


def setup_inputs(seed: int = 0) -> dict:
    key = jax.random.key(seed)
    ks = jax.random.split(key, 26)

    def nrm(k, shape, s):
        return jax.random.normal(k, shape, jnp.float32) * s

    f_off = PROJ_OFFSETS[9]
    b_in = nrm(ks[9], (DEPTH, PROJ_WIDTH), 0.02).at[:, f_off:f_off + M_HEADS].add(jnp.linspace(3.0, 6.0, M_HEADS))
    return {
        'x_prompt': nrm(ks[0], (BATCH, SEQ, D_MODEL), 1.0),
        'x_sample': nrm(ks[1], (DEC_BATCH, DEC_SEQ, D_MODEL), 1.0),
        'cache_k': nrm(ks[2], (DEPTH, DEC_BATCH, PAST_LEN, HEAD_DIM), 1.0),
        'cache_v': nrm(ks[3], (DEPTH, DEC_BATCH, PAST_LEN, HEAD_DIM), 1.0),
        'cache_kidx': nrm(ks[4], (DEPTH, DEC_BATCH, PAST_LEN, IDX_DIM), 1.0),
        'state_C': nrm(ks[5], (DEPTH, DEC_BATCH, M_HEADS, M_DV, M_DK), 1.0),
        'state_n': nrm(ks[6], (DEPTH, DEC_BATCH, M_HEADS, M_DK), 1.0),
        'state_m': nrm(ks[7], (DEPTH, DEC_BATCH, M_HEADS), 1.0),
        'w_in': nrm(ks[8], (DEPTH, D_MODEL, PROJ_WIDTH), D_MODEL ** -0.5),
        'b_in': b_in,
        'w_out': nrm(ks[10], (DEPTH, MIX_WIDTH, D_MODEL), DN_BETA * MIX_WIDTH ** -0.5),
        'mh_norm_g': 1.0 + nrm(ks[11], (DEPTH, M_WIDTH), 0.02),
        'ln1_g': 1.0 + nrm(ks[12], (DEPTH, D_MODEL), 0.02),
        'ln1_b': nrm(ks[13], (DEPTH, D_MODEL), 0.02),
        'router_w': nrm(ks[14], (DEPTH, D_MODEL, N_EXPERTS), D_MODEL ** -0.5),
        'router_b': nrm(ks[15], (DEPTH, N_EXPERTS), 0.01),
        'w_gate': nrm(ks[16], (DEPTH, N_EXPERTS, D_MODEL, D_EXPERT), D_MODEL ** -0.5),
        'b_gate': nrm(ks[17], (DEPTH, N_EXPERTS, D_EXPERT), 0.02),
        'w_up': nrm(ks[18], (DEPTH, N_EXPERTS, D_MODEL, D_EXPERT), D_MODEL ** -0.5),
        'b_up': nrm(ks[19], (DEPTH, N_EXPERTS, D_EXPERT), 0.02),
        'w_down': nrm(ks[20], (DEPTH, N_EXPERTS, D_EXPERT, D_MODEL), DN_BETA * D_EXPERT ** -0.5),
        'b_down': nrm(ks[21], (DEPTH, N_EXPERTS, D_MODEL), 0.02),
        'ln2_g': 1.0 + nrm(ks[22], (DEPTH, D_MODEL), 0.02),
        'ln2_b': nrm(ks[23], (DEPTH, D_MODEL), 0.02),
    }


def reference(x_prompt, x_sample, cache_k, cache_v, cache_kidx, state_C, state_n, state_m,
              w_in, b_in, w_out, mh_norm_g, ln1_g, ln1_b, router_w, router_b,
              w_gate, b_gate, w_up, b_up, w_down, b_down, ln2_g, ln2_b):
    def trunk(x, pos, topk, use_cache):
        B, T, D = x.shape
        rk, rv, ri, sC, sn, sm = [], [], [], [], [], []
        for l in range(DEPTH):
            if use_cache:
                past = (cache_k[l], cache_v[l], cache_kidx[l])
                st0 = (state_C[l], state_n[l], state_m[l])
            else:
                past = None
                st0 = (jnp.zeros((B, M_HEADS, M_DV, M_DK), jnp.float32),
                       jnp.zeros((B, M_HEADS, M_DK), jnp.float32),
                       jnp.zeros((B, M_HEADS), jnp.float32))
            mix, (k_new, v_new, ki_new), (C, n, m) = _mixer(x, pos, w_in[l], b_in[l], w_out[l], mh_norm_g[l],
                                                            past, st0, topk)
            x = _layer_norm(DN_ALPHA * x + mix, ln1_g[l], ln1_b[l])
            ff = _moe(x.reshape(B * T, D), router_w[l], router_b[l], w_gate[l], b_gate[l],
                      w_up[l], b_up[l], w_down[l], b_down[l]).reshape(B, T, D)
            x = _layer_norm(DN_ALPHA * x + ff, ln2_g[l], ln2_b[l])
            rk.append(k_new)
            rv.append(v_new)
            ri.append(ki_new)
            sC.append(C.astype(x.dtype))
            sn.append(n.astype(x.dtype))
            sm.append(m.astype(x.dtype))
        return x, jnp.stack(rk), jnp.stack(rv), jnp.stack(ri), jnp.stack(sC), jnp.stack(sn), jnp.stack(sm)

    past_len = cache_k.shape[2]
    topk_p = min(TOPK_MAX, x_prompt.shape[1] // 4)
    topk_s = min(TOPK_MAX, (past_len + x_sample.shape[1]) // 4)
    pos_p = jnp.arange(x_prompt.shape[1])
    pos_s = past_len + jnp.arange(x_sample.shape[1])
    y_p, k_p, v_p, ki_p, C_p, n_p, m_p = trunk(x_prompt, pos_p, topk_p, False)
    y_s, k_s, v_s, ki_s, C_s, n_s, m_s = trunk(x_sample, pos_s, topk_s, True)
    return (y_p, y_s, k_p, v_p, ki_p, C_p, n_p, m_p, k_s, v_s, ki_s, C_s, n_s, m_s)
```

```python
import functools

import numpy as np
import jax
import jax.numpy as jnp
from jax import lax
from jax.experimental import pallas as pl
from jax.experimental.pallas import tpu as pltpu

F32 = jnp.float32
BF16 = jnp.bfloat16

CHUNK = 64
A_HEADS, HEAD_DIM = 16, 64
IDX_HEADS, IDX_DIM = 8, 64
TOPK_MAX = 256
ROPE_THETA = 10000.0
M_HEADS, M_DK, M_DV = 4, 128, 256
N_EXPERTS, TOP_K = 32, 4
SWIGLU_LIMIT, SWIGLU_ALPHA = 7.0, 1.702
LN_EPS = 1e-5

A_WIDTH = A_HEADS * HEAD_DIM
M_WIDTH = M_HEADS * M_DV
IQ_WIDTH = IDX_HEADS * IDX_DIM
MQ_WIDTH = M_HEADS * M_DK

LANES = 128
SUBLANES = 8
VMEM_LIMIT = 56 * 1024 * 1024

COL_AQ = 0
COL_MV = COL_AQ + A_WIDTH
COL_MO = COL_MV + M_WIDTH
COL_IQ = COL_MO + M_WIDTH
COL_MQ = COL_IQ + IQ_WIDTH
COL_MK = COL_MQ + MQ_WIDTH
COL_SM = COL_MK + MQ_WIDTH
SM_WIDTH = 256
PROJ_COLS = COL_SM + SM_WIDTH
SM_AK, SM_IK, SM_AV = 0, 64, 128
SM_IW = 192
SM_MI = SM_IW + IDX_HEADS
SM_MF = SM_MI + M_HEADS
PROJ_TN = 256

INT_MIN = -2 ** 31
NEG_BIG = -1e30


def _cparams(sem):
    return pltpu.CompilerParams(dimension_semantics=sem, vmem_limit_bytes=VMEM_LIMIT)


def _arrange_cols(w):
    o = np.cumsum([0, A_WIDTH, HEAD_DIM, HEAD_DIM, IQ_WIDTH, IDX_DIM, IDX_HEADS,
                   MQ_WIDTH, MQ_WIDTH, M_WIDTH, M_HEADS, M_HEADS, M_WIDTH])
    seg = lambda i: w[..., o[i]:o[i + 1]]
    used = 3 * 64 + IDX_HEADS + 2 * M_HEADS
    pad = jnp.zeros(w.shape[:-1] + (SM_WIDTH - used,), w.dtype)
    return jnp.concatenate([seg(0), seg(8), seg(11), seg(3), seg(6), seg(7),
                            seg(1), seg(4), seg(2), seg(5), seg(9), seg(10), pad], axis=-1)


def _col_scale():
    s = np.ones((1, PROJ_COLS), np.float32)
    s[:, COL_AQ:COL_AQ + A_WIDTH] = HEAD_DIM ** -0.5
    s[:, COL_IQ:COL_IQ + IQ_WIDTH] = IDX_DIM ** -0.5
    s[:, COL_MQ:COL_MQ + MQ_WIDTH] = M_DK ** -0.5
    s[:, COL_SM + SM_IW:COL_SM + SM_IW + IDX_HEADS] = IDX_HEADS ** -0.5
    return s


def _rope_tables(positions):
    inv = ROPE_THETA ** (-np.arange(0, HEAD_DIM, 2, dtype=np.float64) / HEAD_DIM)
    ang = positions.astype(np.float64)[:, None] * inv[None, :]
    cos = np.cos(ang)
    sin = np.sin(ang)
    cos128 = np.tile(cos, (1, 4))
    sin128 = np.tile(np.concatenate([-sin, sin], axis=1), (1, 2))
    return cos128.astype(np.float32), sin128.astype(np.float32)


def _proj_kernel(x_ref, w_ref, b_ref, s_ref, cos_ref, sin_ref, o_ref, xb_ref, *, full_rope, half_rope):
    j = pl.program_id(1)

    @pl.when(j == 0)
    def _():
        xb_ref[...] = x_ref[...].astype(BF16)

    y = jnp.dot(xb_ref[...], w_ref[...], preferred_element_type=F32) + b_ref[...]
    tn = y.shape[1]

    def rope(v):
        lane = lax.broadcasted_iota(jnp.int32, v.shape, 1)
        first = (lane & 32) == 0
        rot = jnp.where(first, pltpu.roll(v, LANES - 32, 1), pltpu.roll(v, 32, 1))
        return v * cos_ref[...] + rot * sin_ref[...]

    is_full = functools.reduce(jnp.logical_or, [j == f for f in full_rope])
    is_half = j == half_rope

    @pl.when(is_full)
    def _():
        for c in range(tn // LANES):
            sl = slice(c * LANES, (c + 1) * LANES)
            o_ref[:, sl] = rope(y[:, sl]) * s_ref[:, sl]

    @pl.when(is_half)
    def _():
        o_ref[:, :LANES] = rope(y[:, :LANES]) * s_ref[:, :LANES]
        o_ref[:, LANES:] = y[:, LANES:] * s_ref[:, LANES:]

    @pl.when(jnp.logical_not(jnp.logical_or(is_full, is_half)))
    def _():
        o_ref[...] = y * s_ref[...]


def _proj(x, w, b, cos_t, sin_t, *, tm, n_prompt_rows, seq):
    n, d = x.shape
    nblk_p = n_prompt_rows // tm
    per_seq = seq // tm
    full = tuple(range(COL_AQ // PROJ_TN, (COL_AQ + A_WIDTH) // PROJ_TN)) + \
        tuple(range(COL_IQ // PROJ_TN, (COL_IQ + IQ_WIDTH) // PROJ_TN))
    half = COL_SM // PROJ_TN

    def tab_map(i, j):
        return (jnp.where(i < nblk_p, i % per_seq, per_seq + i - nblk_p), 0)

    return pl.pallas_call(
        functools.partial(_proj_kernel, full_rope=full, half_rope=half),
        grid=(n // tm, PROJ_COLS // PROJ_TN),
        in_specs=[
            pl.BlockSpec((tm, d), lambda i, j: (i, 0)),
            pl.BlockSpec((d, PROJ_TN), lambda i, j: (0, j)),
            pl.BlockSpec((1, PROJ_TN), lambda i, j: (0, j)),
            pl.BlockSpec((1, PROJ_TN), lambda i, j: (0, j)),
            pl.BlockSpec((tm, LANES), tab_map),
            pl.BlockSpec((tm, LANES), tab_map),
        ],
        out_specs=pl.BlockSpec((tm, PROJ_TN), lambda i, j: (i, j)),
        out_shape=jax.ShapeDtypeStruct((n, PROJ_COLS), F32),
        scratch_shapes=[pltpu.VMEM((tm, d), BF16)],
        compiler_params=_cparams(("parallel", "arbitrary")),
        name="in_proj",
    )(x, w, b, jnp.asarray(_col_scale()), cos_t, sin_t)


def _dsa_kernel(q_ref, iq_ref, sm_ref, k_ref, ki_ref, v_ref, o_ref,
                q2_ref, iq2_ref, key_ref, sel_ref, m_ref, l_ref, acc_ref, *,
                tq, tk, n_keys, q_pos0, topk, causal):
    i = pl.program_id(1)
    nkb_all = key_ref.shape[0]
    nkb = (i + 1) if causal else nkb_all
    qbase = q_pos0 + i * tq

    for h in range(A_HEADS):
        q2_ref[h * tq:(h + 1) * tq, :] = q_ref[:, h * HEAD_DIM:(h + 1) * HEAD_DIM].astype(BF16)
    for h in range(IDX_HEADS):
        iq2_ref[h * tq:(h + 1) * tq, :] = iq_ref[:, h * IDX_DIM:(h + 1) * IDX_DIM].astype(BF16)
    w = sm_ref[:, SM_IW:SM_IW + IDX_HEADS]

    qpos = qbase + lax.broadcasted_iota(jnp.int32, (tq, 1), 0)
    nt = (((1,), (1,)), ((), ()))

    def score_block(kb, carry):
        ks = pl.multiple_of(kb * tk, tk)
        kib = ki_ref[pl.ds(ks, tk), :].astype(BF16)
        d = lax.dot_general(iq2_ref[...], kib, nt, preferred_element_type=F32)
        sc = jnp.zeros((tq, tk), F32)
        for h in range(IDX_HEADS):
            sc = sc + w[:, h:h + 1] * jnp.maximum(d[h * tq:(h + 1) * tq, :], 0.0)
        kpos = ks + lax.broadcasted_iota(jnp.int32, (1, tk), 1)
        adm = jnp.logical_and((kpos >> 6) <= (qpos >> 6), kpos < n_keys)
        bits = lax.bitcast_convert_type(sc, jnp.int32)
        key = bits ^ ((bits >> 31) & 0x7FFFFFFF)
        key_ref[kb] = jnp.where(adm, key, INT_MIN)
        return carry

    lax.fori_loop(0, nkb, score_block, 0)

    def count(pred):
        def body(kb, acc):
            return acc + jnp.sum(pred(key_ref[kb]).astype(F32), axis=1, keepdims=True)
        return lax.fori_loop(0, nkb, body, jnp.zeros((tq, 1), F32))

    kf = float(topk)

    def search(it, r):
        cand_b = r | (jnp.int32(1) << (31 - it))
        cand = cand_b ^ INT_MIN
        cnt = count(lambda kk: kk >= cand)
        return jnp.where(cnt >= kf, cand_b, r)

    r = lax.fori_loop(0, 32, search, jnp.zeros((tq, 1), jnp.int32))
    thr = r ^ INT_MIN
    need = kf - count(lambda kk: kk > thr)

    m_ref[...] = jnp.full(m_ref.shape, NEG_BIG, F32)
    l_ref[...] = jnp.zeros(l_ref.shape, F32)
    acc_ref[...] = jnp.zeros(acc_ref.shape, F32)

    rowi = lax.broadcasted_iota(jnp.int32, (tk, tk), 0)
    coli = lax.broadcasted_iota(jnp.int32, (tk, tk), 1)
    before = jnp.where(rowi < coli, 1.0, 0.0).astype(BF16)

    def attend_block(kb, off):
        kk = key_ref[kb]
        eq = kk == thr
        eqf = jnp.where(eq, 1.0, 0.0)
        rank = jnp.dot(eqf.astype(BF16), before, preferred_element_type=F32) + off
        sel = jnp.logical_or(kk > thr, jnp.logical_and(eq, rank < need))
        sel = jnp.logical_and(sel, kk != INT_MIN)
        sel_ref[...] = jnp.where(sel, 1.0, 0.0)
        ks = pl.multiple_of(kb * tk, tk)
        kblk = k_ref[pl.ds(ks, tk), :].astype(BF16)
        vblk = v_ref[pl.ds(ks, tk), :].astype(BF16)

        def head(h, c):
            rs = pl.multiple_of(h * tq, tq)
            rows = pl.ds(rs, tq)
            selw = sel_ref[...]
            s = lax.dot_general(q2_ref[rows, :], kblk, nt, preferred_element_type=F32)
            s = jnp.where(selw > 0.0, s, NEG_BIG)
            m_old = m_ref[rows, :]
            m_new = jnp.maximum(m_old, jnp.max(s, axis=1, keepdims=True))
            p = jnp.exp(s - m_new) * selw
            alpha = jnp.exp(m_old - m_new)
            l_ref[rows, :] = alpha * l_ref[rows, :] + jnp.sum(p, axis=1, keepdims=True)
            acc_ref[rows, :] = alpha * acc_ref[rows, :] + jnp.dot(p.astype(BF16), vblk,
                                                                  preferred_element_type=F32)
            m_ref[rows, :] = m_new
            return c

        lax.fori_loop(0, A_HEADS, head, 0)
        return off + jnp.sum(eqf, axis=1, keepdims=True)

    lax.fori_loop(0, nkb, attend_block, jnp.zeros((tq, 1), F32))

    for h in range(A_HEADS):
        rows = slice(h * tq, (h + 1) * tq)
        o_ref[:, h * HEAD_DIM:(h + 1) * HEAD_DIM] = (acc_ref[rows, :] / l_ref[rows, :]).astype(o_ref.dtype)


def _dsa(proj, row0, nb, t, k, ki, v, *, q_pos0, n_keys, tq, topk, causal):
    tk = 256
    s_pad = k.shape[1]
    nq = t // tq
    rb0 = row0 // tq
    if causal:
        assert tq == tk and tq % CHUNK == 0
    kern = functools.partial(_dsa_kernel, tq=tq, tk=tk, n_keys=n_keys, q_pos0=q_pos0,
                             topk=topk, causal=causal)
    row_map = lambda c: (lambda b, i: (rb0 + b * nq + i, c))
    kv_spec = pl.BlockSpec((None, s_pad, HEAD_DIM), lambda b, i: (b, 0, 0))
    return pl.pallas_call(
        kern,
        grid=(nb, nq),
        in_specs=[
            pl.BlockSpec((tq, A_WIDTH), row_map(COL_AQ // A_WIDTH)),
            pl.BlockSpec((tq, IQ_WIDTH), row_map(COL_IQ // IQ_WIDTH)),
            pl.BlockSpec((tq, SM_WIDTH), row_map(COL_SM // SM_WIDTH)),
            kv_spec, kv_spec, kv_spec,
        ],
        out_specs=pl.BlockSpec((tq, A_WIDTH), lambda b, i: (b * nq + i, 0)),
        out_shape=jax.ShapeDtypeStruct((nb * t, A_WIDTH), BF16),
        scratch_shapes=[
            pltpu.VMEM((A_HEADS * tq, HEAD_DIM), BF16),
            pltpu.VMEM((IDX_HEADS * tq, IDX_DIM), BF16),
            pltpu.VMEM((s_pad // tk, tq, tk), jnp.int32),
            pltpu.VMEM((tq, tk), F32),
            pltpu.VMEM((A_HEADS * tq, 1), F32),
            pltpu.VMEM((A_HEADS * tq, 1), F32),
            pltpu.VMEM((A_HEADS * tq, HEAD_DIM), F32),
        ],
        compiler_params=_cparams(("parallel", "arbitrary")),
        name="dsa_causal" if causal else "dsa_cached",
    )(proj, proj, proj, k, ki, v)


def _log_sigmoid(x):
    return jnp.minimum(x, 0.0) - jnp.log(1.0 + jnp.exp(-jnp.abs(x)))


def _mlstm_kernel(q_ref, k_ref, v_ref, og_ref, sm_ref, gt_ref, c0_ref, n0_ref, m0_ref, gn_ref,
                  h_ref, c_out, n_out, m_out, ct_s, n_s, m_s, *, chunk, n_chunks):
    c = pl.program_id(1)
    L = chunk

    @pl.when(c == 0)
    def _():
        for h in range(M_HEADS):
            ct_s[h] = c0_ref[h].T
        n_s[...] = n0_ref[...]
        m_s[...] = m0_ref[...]

    ri = lax.broadcasted_iota(jnp.int32, (L, L), 0)
    ci = lax.broadcasted_iota(jnp.int32, (L, L), 1)
    tril = ci <= ri
    lower = jnp.where(tril, 1.0, 0.0)
    upper = jnp.where(ri <= ci, 1.0, 0.0)
    hi = lax.Precision.HIGHEST
    nt = (((1,), (1,)), ((), ()))

    for ch in range(n_chunks):
        rows = slice(ch * L, (ch + 1) * L)
        i_col = sm_ref[rows, SM_MI:SM_MI + M_HEADS]
        f_col = _log_sigmoid(sm_ref[rows, SM_MF:SM_MF + M_HEADS])
        i_row = gt_ref[0:M_HEADS, rows]
        f_row = _log_sigmoid(gt_ref[M_HEADS:2 * M_HEADS, rows])
        b_col = jnp.dot(lower, f_col, precision=hi, preferred_element_type=F32)
        b_row = jnp.dot(f_row, upper, precision=hi, preferred_element_type=F32)
        for h in range(M_HEADS):
            bc = b_col[:, h:h + 1]
            br = b_row[h:h + 1, :]
            ir = i_row[h:h + 1, :]
            ic = i_col[:, h:h + 1]
            m = m_s[h:h + 1, 0:1]
            a = bc + m
            dm = jnp.where(tril, bc - br + ir, -jnp.inf)
            mt = jnp.maximum(a, jnp.max(dm, axis=1, keepdims=True))
            inter = jnp.exp(a - mt)
            qh = q_ref[rows, h * M_DK:(h + 1) * M_DK]
            kh = k_ref[rows, h * M_DK:(h + 1) * M_DK]
            vh = v_ref[rows, h * M_DV:(h + 1) * M_DV]
            qb = qh.astype(BF16)
            kb = kh.astype(BF16)
            s = lax.dot_general(qb, kb, nt, preferred_element_type=F32) * jnp.exp(dm - mt)
            ct = ct_s[h]
            num = jnp.dot(s.astype(BF16), vh.astype(BF16), preferred_element_type=F32) + \
                inter * jnp.dot(qb, ct.astype(BF16), preferred_element_type=F32)
            nrow = n_s[h:h + 1, :]
            den = jnp.sum(s, axis=1, keepdims=True) + inter * jnp.sum(qh * nrow, axis=1, keepdims=True)
            hc = num / jnp.maximum(jnp.abs(den), jnp.exp(-mt))
            bl = bc[L - 1:L, :]
            g_c = bl - bc + ic
            g_r = bl - br + ir
            m_new = jnp.maximum(bl + m, jnp.max(g_r, axis=1, keepdims=True))
            decay = jnp.exp(bl + m - m_new)
            ws = jnp.exp(g_c - m_new)
            kt = kh.T.astype(BF16)
            ct_s[h] = decay * ct + jnp.dot(kt, (ws * vh).astype(BF16), preferred_element_type=F32)
            n_s[h:h + 1, :] = decay * nrow + jnp.sum(ws * kh, axis=0, keepdims=True)
            m_s[h:h + 1, :] = jnp.broadcast_to(m_new, (1, LANES))
            mu = jnp.mean(hc, axis=1, keepdims=True)
            hz = hc - mu
            var = jnp.mean(hz * hz, axis=1, keepdims=True)
            cols = slice(h * M_DV, (h + 1) * M_DV)
            hn = hz * lax.rsqrt(var + LN_EPS) * gn_ref[:, cols]
            h_ref[rows, cols] = (hn * jax.nn.sigmoid(og_ref[rows, cols])).astype(h_ref.dtype)

    @pl.when(c == pl.num_programs(1) - 1)
    def _():
        for h in range(M_HEADS):
            c_out[h] = ct_s[h].T
        n_out[...] = n_s[...]
        m_out[...] = m_s[...]


def _mlstm(proj, row0, nb, t, gates_t, c0, n0, m0b, gnorm):
    chunk = CHUNK if t % CHUNK == 0 else t
    tc = min(t, 256)
    nct = t // tc
    rb0 = row0 // tc
    row_map = lambda col: (lambda b, c: (rb0 + b * nct + c, col))
    kern = functools.partial(_mlstm_kernel, chunk=chunk, n_chunks=tc // chunk)
    return pl.pallas_call(
        kern,
        grid=(nb, nct),
        in_specs=[
            pl.BlockSpec((tc, MQ_WIDTH), row_map(COL_MQ // MQ_WIDTH)),
            pl.BlockSpec((tc, MQ_WIDTH), row_map(COL_MK // MQ_WIDTH)),
            pl.BlockSpec((tc, M_WIDTH), row_map(COL_MV // M_WIDTH)),
            pl.BlockSpec((tc, M_WIDTH), row_map(COL_MO // M_WIDTH)),
            pl.BlockSpec((tc, SM_WIDTH), row_map(COL_SM // SM_WIDTH)),
            pl.BlockSpec((None, 2 * M_HEADS, tc), lambda b, c: (b, 0, c)),
            pl.BlockSpec((None, M_HEADS, M_DV, M_DK), lambda b, c: (b, 0, 0, 0)),
            pl.BlockSpec((None, M_HEADS, M_DK), lambda b, c: (b, 0, 0)),
            pl.BlockSpec((None, SUBLANES, LANES), lambda b, c: (b, 0, 0)),
            pl.BlockSpec((1, M_WIDTH), lambda b, c: (0, 0)),
        ],
        out_specs=[
            pl.BlockSpec((tc, M_WIDTH), lambda b, c: (b * nct + c, 0)),
            pl.BlockSpec((None, M_HEADS, M_DV, M_DK), lambda b, c: (b, 0, 0, 0)),
            pl.BlockSpec((None, M_HEADS, M_DK), lambda b, c: (b, 0, 0)),
            pl.BlockSpec((None, SUBLANES, LANES), lambda b, c: (b, 0, 0)),
        ],
        out_shape=[
            jax.ShapeDtypeStruct((nb * t, M_WIDTH), BF16),
            jax.ShapeDtypeStruct((nb, M_HEADS, M_DV, M_DK), F32),
            jax.ShapeDtypeStruct((nb, M_HEADS, M_DK), F32),
            jax.ShapeDtypeStruct((nb, SUBLANES, LANES), F32),
        ],
        scratch_shapes=[
            pltpu.VMEM((M_HEADS, M_DK, M_DV), F32),
            pltpu.VMEM((M_HEADS, M_DK), F32),
            pltpu.VMEM((SUBLANES, LANES), F32),
        ],
        compiler_params=_cparams(("parallel", "arbitrary")),
        name="mlstm_chunked" if nct > 1 else "mlstm_single",
    )(proj, proj, proj, proj, proj, gates_t, c0, n0, m0b, gnorm)


def _layer_norm(z, g, b):
    mu = jnp.mean(z, axis=1, keepdims=True)
    zc = z - mu
    var = jnp.mean(zc * zc, axis=1, keepdims=True)
    return zc * lax.rsqrt(var + LN_EPS) * g + b


def _mix_kernel(a_ref, h_ref, x_ref, wa_ref, wh_ref, g_ref, b_ref, rw_ref, rb_ref,
                x1_ref, x1t_ref, ti_ref, tg_ref, *, alpha, rpt):
    y = jnp.dot(a_ref[...], wa_ref[...], preferred_element_type=F32) + \
        jnp.dot(h_ref[...], wh_ref[...], preferred_element_type=F32)
    x1 = _layer_norm(alpha * x_ref[...] + y, g_ref[...], b_ref[...])
    x1_ref[...] = x1
    tm = x1.shape[0]
    for c in range(rpt):
        x1t_ref[pl.ds(c, tm, stride=rpt), :] = x1[:, c * LANES:(c + 1) * LANES]
    logits = jnp.dot(x1.astype(BF16), rw_ref[...], preferred_element_type=F32) + rb_ref[...]
    lane = lax.broadcasted_iota(jnp.int32, logits.shape, 1)
    lanef = lane.astype(F32)
    cur = jnp.where(lane < N_EXPERTS, logits, -jnp.inf)
    vals, idxs = [], []
    for _ in range(TOP_K):
        mx = jnp.max(cur, axis=1, keepdims=True)
        ix = jnp.min(jnp.where(cur == mx, lanef, float(LANES)), axis=1, keepdims=True)
        vals.append(mx)
        idxs.append(ix)
        cur = jnp.where(lanef == ix, -jnp.inf, cur)
    es = [jnp.exp(v - vals[0]) for v in vals]
    tot = functools.reduce(lambda p, q: p + q, es)
    ti = jnp.zeros(logits.shape, F32)
    tg = jnp.zeros(logits.shape, F32)
    for j in range(TOP_K):
        ti = jnp.where(lane == j, idxs[j], ti)
        tg = jnp.where(lane == j, es[j] / tot, tg)
    ti_ref[...] = ti[:, :SUBLANES].astype(jnp.int32)
    tg_ref[...] = tg[:, :SUBLANES]


def _mix(attn, hm, x, wa, wh, g, b, rw, rb, *, alpha, tm):
    n, d = x.shape
    rpt = d // LANES
    kern = functools.partial(_mix_kernel, alpha=alpha, rpt=rpt)
    const = lambda shape: pl.BlockSpec(shape, lambda i: (0,) * len(shape))
    return pl.pallas_call(
        kern,
        grid=(n // tm,),
        in_specs=[
            pl.BlockSpec((tm, A_WIDTH), lambda i: (i, 0)),
            pl.BlockSpec((tm, M_WIDTH), lambda i: (i, 0)),
            pl.BlockSpec((tm, d), lambda i: (i, 0)),
            const((A_WIDTH, d)), const((M_WIDTH, d)),
            const((1, d)), const((1, d)),
            const((d, LANES)), const((1, LANES)),
        ],
        out_specs=[
            pl.BlockSpec((tm, d), lambda i: (i, 0)),
            pl.BlockSpec((tm * rpt, LANES), lambda i: (i, 0)),
            pl.BlockSpec((tm, SUBLANES), lambda i: (i, 0)),
            pl.BlockSpec((tm, SUBLANES), lambda i: (i, 0)),
        ],
        out_shape=[
            jax.ShapeDtypeStruct((n, d), F32),
            jax.ShapeDtypeStruct((n * rpt, LANES), F32),
            jax.ShapeDtypeStruct((n, SUBLANES), jnp.int32),
            jax.ShapeDtypeStruct((n, SUBLANES), F32),
        ],
        compiler_params=_cparams(("parallel",)),
        name="mix_ln_router",
    )(attn, hm, x, wa, wh, g, b, rw, rb)


def _gather_pipeline(blk, n_active, idx_hbm, src_hbm, idx_s, buf, sem_i, sem_r, *, n_rows, rpt):
    slot = lax.rem(blk, 2)
    nslot = 1 - slot

    def idx_copy(b, s):
        return pltpu.make_async_copy(idx_hbm.at[pl.ds(pl.multiple_of(b * SUBLANES, SUBLANES), SUBLANES), :],
                                     idx_s.at[s], sem_i.at[s])

    def issue_rows(s):
        def body(r, c):
            src_row = idx_s[s, r // LANES, r % LANES]
            pltpu.make_async_copy(
                src_hbm.at[pl.ds(pl.multiple_of(src_row * rpt, rpt), rpt), :],
                buf.at[s, pl.ds(pl.multiple_of(r * rpt, rpt), rpt), :],
                sem_r.at[s]).start()
            return c
        lax.fori_loop(0, n_rows, body, 0, unroll=8)

    @pl.when(blk == 0)
    def _():
        first = idx_copy(0, 0)
        first.start()
        first.wait()
        issue_rows(0)

        @pl.when(n_active > 1)
        def _():
            idx_copy(1, 1).start()

    @pl.when(blk + 1 < n_active)
    def _():
        idx_copy(blk + 1, nslot).wait()
        issue_rows(nslot)

        @pl.when(blk + 2 < n_active)
        def _():
            idx_copy(blk + 2, slot).start()

    pltpu.make_async_copy(buf.at[slot], buf.at[slot], sem_r.at[slot]).wait()
    return slot


def _moe_kernel(be_ref, nu_ref, idx_hbm, x_hbm, wg_ref, bg_ref, wu_ref, bu_ref, wd_ref, bd_ref,
                y_ref, idx_s, xbuf, xb, sem_i, sem_r, *, tm, rpt):
    blk = pl.program_id(0)
    n_active = nu_ref[0]

    @pl.when(blk < n_active)
    def _():
        slot = _gather_pipeline(blk, n_active, idx_hbm, x_hbm, idx_s, xbuf, sem_i, sem_r,
                                n_rows=tm, rpt=rpt)
        for c in range(rpt):
            xb[:, c * LANES:(c + 1) * LANES] = xbuf[slot, pl.ds(c, tm, stride=rpt), :].astype(BF16)
        xv = xb[...]
        g = jnp.minimum(jnp.dot(xv, wg_ref[...], preferred_element_type=F32) + bg_ref[...], SWIGLU_LIMIT)
        u = jnp.clip(jnp.dot(xv, wu_ref[...], preferred_element_type=F32) + bu_ref[...],
                     -SWIGLU_LIMIT, SWIGLU_LIMIT)
        hdn = g * jax.nn.sigmoid(SWIGLU_ALPHA * g) * (u + 1.0)
        y = jnp.dot(hdn.astype(BF16), wd_ref[...], preferred_element_type=F32) + bd_ref[...]
        for c in range(rpt):
            y_ref[pl.ds(c, tm, stride=rpt), :] = y[:, c * LANES:(c + 1) * LANES]

    @pl.when(blk >= n_active)
    def _():
        y_ref[...] = jnp.zeros(y_ref.shape, y_ref.dtype)


def _moe_experts(block_expert, n_active, idx_tiles, x1t, wg, bg, wu, bu, wd, bd, *, tm, d):
    n_blocks = block_expert.shape[0]
    rpt = d // LANES
    de = wg.shape[-1]
    kern = functools.partial(_moe_kernel, tm=tm, rpt=rpt)
    wmap = lambda i, be, nu: (be[i], 0, 0)
    grid_spec = pltpu.PrefetchScalarGridSpec(
        num_scalar_prefetch=2,
        grid=(n_blocks,),
        in_specs=[
            pl.BlockSpec(memory_space=pl.ANY),
            pl.BlockSpec(memory_space=pl.ANY),
            pl.BlockSpec((None, d, de), wmap), pl.BlockSpec((None, 1, de), wmap),
            pl.BlockSpec((None, d, de), wmap), pl.BlockSpec((None, 1, de), wmap),
            pl.BlockSpec((None, de, d), wmap), pl.BlockSpec((None, 1, d), wmap),
        ],
        out_specs=pl.BlockSpec((tm * rpt, LANES), lambda i, be, nu: (i, 0)),
        scratch_shapes=[
            pltpu.SMEM((2, SUBLANES, LANES), jnp.int32),
            pltpu.VMEM((2, tm * rpt, LANES), F32),
            pltpu.VMEM((tm, d), BF16),
            pltpu.SemaphoreType.DMA((2,)),
            pltpu.SemaphoreType.DMA((2,)),
        ],
    )
    return pl.pallas_call(
        kern,
        grid_spec=grid_spec,
        out_shape=jax.ShapeDtypeStruct((n_blocks * tm * rpt, LANES), F32),
        compiler_params=_cparams(("arbitrary",)),
        name="moe_experts",
    )(block_expert, n_active, idx_tiles, x1t, wg, bg, wu, bu, wd, bd)


def _combine_kernel(idx_hbm, y_hbm, x_ref, tg_ref, g_ref, b_ref, o_ref, idx_s, ybuf, sem_i, sem_r,
                    *, tm, rpt, alpha):
    blk = pl.program_id(0)
    slot = _gather_pipeline(blk, pl.num_programs(0), idx_hbm, y_hbm, idx_s, ybuf, sem_i, sem_r,
                            n_rows=TOP_K * tm, rpt=rpt)
    gates = tg_ref[...]
    cols = []
    for c in range(rpt):
        acc = alpha * x_ref[:, c * LANES:(c + 1) * LANES]
        for j in range(TOP_K):
            acc = acc + gates[:, j:j + 1] * ybuf[slot, pl.ds(j * tm * rpt + c, tm, stride=rpt), :]
        cols.append(acc)
    z = jnp.concatenate(cols, axis=1)
    o_ref[...] = _layer_norm(z, g_ref[...], b_ref[...])


def _combine(idx_tiles, yb, x1, tg, g, b, *, alpha, tm):
    n, d = x1.shape
    rpt = d // LANES
    kern = functools.partial(_combine_kernel, tm=tm, rpt=rpt, alpha=alpha)
    return pl.pallas_call(
        kern,
        grid=(n // tm,),
        in_specs=[
            pl.BlockSpec(memory_space=pl.ANY),
            pl.BlockSpec(memory_space=pl.ANY),
            pl.BlockSpec((tm, d), lambda i: (i, 0)),
            pl.BlockSpec((tm, SUBLANES), lambda i: (i, 0)),
            pl.BlockSpec((1, d), lambda i: (0, 0)),
            pl.BlockSpec((1, d), lambda i: (0, 0)),
        ],
        out_specs=pl.BlockSpec((tm, d), lambda i: (i, 0)),
        out_shape=jax.ShapeDtypeStruct((n, d), F32),
        scratch_shapes=[
            pltpu.SMEM((2, SUBLANES, LANES), jnp.int32),
            pltpu.VMEM((2, TOP_K * tm * rpt, LANES), F32),
            pltpu.SemaphoreType.DMA((2,)),
            pltpu.SemaphoreType.DMA((2,)),
        ],
        compiler_params=_cparams(("arbitrary",)),
        name="moe_combine_ln",
    )(idx_tiles, yb, x1, tg, g, b)


def _routing(top_idx, *, tm_e, tm_c):
    n = top_idx.shape[0]
    a = n * TOP_K
    n_blocks = (a + N_EXPERTS * (tm_e - 1) + tm_e - 1) // tm_e
    r = n_blocks * tm_e
    e_flat = top_idx.reshape(-1)
    order = jnp.argsort(e_flat, stable=True)
    e_sorted = e_flat[order]
    counts = jnp.bincount(e_flat, length=N_EXPERTS)
    padded = ((counts + tm_e - 1) // tm_e) * tm_e
    pad_end = jnp.cumsum(padded)
    pad_start = pad_end - padded
    grp_start = jnp.cumsum(counts) - counts
    dest = (pad_start[e_sorted] + (jnp.arange(a) - grp_start[e_sorted])).astype(jnp.int32)
    rows_tok = jnp.zeros((r,), jnp.int32).at[dest].set((order // TOP_K).astype(jnp.int32))
    pos = jnp.zeros((a,), jnp.int32).at[order].set(dest).reshape(n, TOP_K)
    n_active = (pad_end[-1] // tm_e).astype(jnp.int32).reshape(1)
    blk_start = jnp.minimum(jnp.arange(n_blocks), n_active[0] - 1) * tm_e
    block_expert = jnp.minimum(jnp.searchsorted(pad_end, blk_start, side='right'),
                               N_EXPERTS - 1).astype(jnp.int32)

    def tiles(idx2d):
        nb, per = idx2d.shape
        padw = SUBLANES * LANES - per
        return jnp.pad(idx2d, ((0, 0), (0, padw))).reshape(nb * SUBLANES, LANES)

    idx_e = tiles(rows_tok.reshape(n_blocks, tm_e))
    idx_c = tiles(pos.reshape(n // tm_c, tm_c, TOP_K).transpose(0, 2, 1).reshape(n // tm_c, TOP_K * tm_c))
    return block_expert, n_active, idx_e, idx_c


def kernel(x_prompt, x_sample, cache_k, cache_v, cache_kidx, state_C, state_n, state_m,
           w_in, b_in, w_out, mh_norm_g, ln1_g, ln1_b, router_w, router_b,
           w_gate, b_gate, w_up, b_up, w_down, b_down, ln2_g, ln2_b):
    depth = w_in.shape[0]
    bp, tp, d = x_prompt.shape
    bs, ts, _ = x_sample.shape
    past = cache_k.shape[2]
    n_p, n_s = bp * tp, bs * ts
    n = n_p + n_s
    alpha = float((2.0 * depth) ** 0.25)
    tm = 512
    tm_e, tm_c = 256, 128
    tq_p = 256
    assert tp % tm == 0 and n_s % tm == 0 and n % tm_c == 0 and tp % tq_p == 0
    topk_p = min(TOPK_MAX, tp // 4)
    topk_s = min(TOPK_MAX, (past + ts) // 4)
    s_keys = past + ts
    s_pad = -(-s_keys // 256) * 256

    pos = np.concatenate([np.arange(tp), np.tile(past + np.arange(ts), bs)])
    cos_t, sin_t = (jnp.asarray(t) for t in _rope_tables(pos))

    x = jnp.concatenate([x_prompt.reshape(n_p, d), x_sample.reshape(n_s, d)], axis=0)
    w_in_k = _arrange_cols(w_in).astype(BF16)
    b_in_k = _arrange_cols(b_in).reshape(depth, 1, PROJ_COLS)
    w_out_k = w_out.astype(BF16)
    rw_k = jnp.pad(router_w, ((0, 0), (0, 0), (0, LANES - N_EXPERTS))).astype(BF16)
    rb_k = jnp.pad(router_b, ((0, 0), (0, LANES - N_EXPERTS))).reshape(depth, 1, LANES)
    wg_k, wu_k, wd_k = w_gate.astype(BF16), w_up.astype(BF16), w_down.astype(BF16)
    de = w_gate.shape[-1]

    zeros_c = jnp.zeros((bp, M_HEADS, M_DV, M_DK), F32)
    zeros_n = jnp.zeros((bp, M_HEADS, M_DK), F32)
    zeros_m = jnp.zeros((bp, SUBLANES, LANES), F32)

    outs = {name: [] for name in ("kp", "vp", "ip", "cp", "np", "mp", "ks", "vs", "is", "cs", "ns", "ms")}
    for l in range(depth):
        proj = _proj(x, w_in_k[l], b_in_k[l], cos_t, sin_t, tm=tm, n_prompt_rows=n_p, seq=tp)
        small = proj[:, COL_SM:COL_SM + SM_WIDTH]
        k_new = small[:, SM_AK:SM_AK + HEAD_DIM]
        ki_new = small[:, SM_IK:SM_IK + IDX_DIM]
        v_new = small[:, SM_AV:SM_AV + HEAD_DIM]
        gates = small[:, SM_MI:SM_MI + 2 * M_HEADS]
        kp, ks = k_new[:n_p].reshape(bp, tp, HEAD_DIM), k_new[n_p:].reshape(bs, ts, HEAD_DIM)
        ip, is_ = ki_new[:n_p].reshape(bp, tp, IDX_DIM), ki_new[n_p:].reshape(bs, ts, IDX_DIM)
        vp, vs = v_new[:n_p].reshape(bp, tp, HEAD_DIM), v_new[n_p:].reshape(bs, ts, HEAD_DIM)
        gt_p = gates[:n_p].reshape(bp, tp, 2 * M_HEADS).transpose(0, 2, 1)
        gt_s = gates[n_p:].reshape(bs, ts, 2 * M_HEADS).transpose(0, 2, 1)

        attn_p = _dsa(proj, 0, bp, tp, kp, ip, vp, q_pos0=0, n_keys=tp, tq=tq_p, topk=topk_p, causal=True)
        padk = lambda c, nw: jnp.pad(jnp.concatenate([c, nw], axis=1), ((0, 0), (0, s_pad - s_keys), (0, 0)))
        attn_s = _dsa(proj, n_p, bs, ts, padk(cache_k[l], ks), padk(cache_kidx[l], is_), padk(cache_v[l], vs),
                      q_pos0=past, n_keys=s_keys, tq=ts, topk=topk_s, causal=False)
        gn = mh_norm_g[l].reshape(1, M_WIDTH)
        hm_p, c_p, nn_p, m_p = _mlstm(proj, 0, bp, tp, gt_p, zeros_c, zeros_n, zeros_m, gn)
        m0s = jnp.broadcast_to(jnp.pad(state_m[l], ((0, 0), (0, SUBLANES - M_HEADS)))[:, :, None],
                               (bs, SUBLANES, LANES))
        hm_s, c_s, nn_s, m_s = _mlstm(proj, n_p, bs, ts, gt_s, state_C[l], state_n[l], m0s, gn)

        attn = jnp.concatenate([attn_p, attn_s], axis=0)
        hm = jnp.concatenate([hm_p, hm_s], axis=0)
        x1, x1t, top_i, top_g = _mix(attn, hm, x, w_out_k[l, :A_WIDTH], w_out_k[l, A_WIDTH:],
                                     ln1_g[l].reshape(1, d), ln1_b[l].reshape(1, d), rw_k[l], rb_k[l],
                                     alpha=alpha, tm=tm)
        block_expert, n_active, idx_e, idx_c = _routing(top_i[:, :TOP_K], tm_e=tm_e, tm_c=tm_c)
        yb = _moe_experts(block_expert, n_active, idx_e, x1t,
                          wg_k[l], b_gate[l].reshape(N_EXPERTS, 1, de),
                          wu_k[l], b_up[l].reshape(N_EXPERTS, 1, de),
                          wd_k[l], b_down[l].reshape(N_EXPERTS, 1, d), tm=tm_e, d=d)
        x = _combine(idx_c, yb, x1, top_g, ln2_g[l].reshape(1, d), ln2_b[l].reshape(1, d),
                     alpha=alpha, tm=tm_c)

        for key, val in (("kp", kp), ("vp", vp), ("ip", ip), ("cp", c_p), ("np", nn_p),
                         ("mp", m_p[:, :M_HEADS, 0]), ("ks", ks), ("vs", vs), ("is", is_),
                         ("cs", c_s), ("ns", nn_s), ("ms", m_s[:, :M_HEADS, 0])):
            outs[key].append(val)

    st = {key: jnp.stack(val) for key, val in outs.items()}
    y_p = x[:n_p].reshape(bp, tp, d)
    y_s = x[n_p:].reshape(bs, ts, d)
    return (y_p, y_s, st["kp"], st["vp"], st["ip"], st["cp"], st["np"], st["mp"],
            st["ks"], st["vs"], st["is"], st["cs"], st["ns"], st["ms"])
```

```python
import functools

import numpy as np
import jax
import jax.numpy as jnp
from jax import lax
from jax.experimental import pallas as pl
from jax.experimental.pallas import tpu as pltpu

F32 = jnp.float32
BF16 = jnp.bfloat16

CHUNK = 64
A_HEADS, HEAD_DIM = 16, 64
IDX_HEADS, IDX_DIM = 8, 64
TOPK_MAX = 256
ROPE_THETA = 10000.0
M_HEADS, M_DK, M_DV = 4, 128, 256
N_EXPERTS, TOP_K = 32, 4
SWIGLU_LIMIT, SWIGLU_ALPHA = 7.0, 1.702
LN_EPS = 1e-5

A_WIDTH = A_HEADS * HEAD_DIM
M_WIDTH = M_HEADS * M_DV
IQ_WIDTH = IDX_HEADS * IDX_DIM
MQ_WIDTH = M_HEADS * M_DK

LANES = 128
LANE_BITS = 7
SUBLANES = 8
VMEM_LIMIT = 56 * 1024 * 1024

COL_AQ = 0
COL_MV = COL_AQ + A_WIDTH
COL_MO = COL_MV + M_WIDTH
COL_IQ = COL_MO + M_WIDTH
COL_MQ = COL_IQ + IQ_WIDTH
COL_MK = COL_MQ + MQ_WIDTH
COL_SM = COL_MK + MQ_WIDTH
SM_WIDTH = 256
PROJ_COLS = COL_SM + SM_WIDTH
SM_AK, SM_IK, SM_AV = 0, 64, 128
SM_IW = 192
SM_MI = SM_IW + IDX_HEADS
SM_MF = SM_MI + M_HEADS
PROJ_TN = 256

INT_MIN = -2 ** 31
NEG_BIG = -1e30


def _cparams(sem):
    return pltpu.CompilerParams(dimension_semantics=sem, vmem_limit_bytes=VMEM_LIMIT)


def _arrange_cols(w):
    o = np.cumsum([0, A_WIDTH, HEAD_DIM, HEAD_DIM, IQ_WIDTH, IDX_DIM, IDX_HEADS,
                   MQ_WIDTH, MQ_WIDTH, M_WIDTH, M_HEADS, M_HEADS, M_WIDTH])
    seg = lambda i: w[..., o[i]:o[i + 1]]
    used = 3 * 64 + IDX_HEADS + 2 * M_HEADS
    pad = jnp.zeros(w.shape[:-1] + (SM_WIDTH - used,), w.dtype)
    return jnp.concatenate([seg(0), seg(8), seg(11), seg(3), seg(6), seg(7),
                            seg(1), seg(4), seg(2), seg(5), seg(9), seg(10), pad], axis=-1)


def _col_scale():
    s = np.ones((1, PROJ_COLS), np.float32)
    s[:, COL_AQ:COL_AQ + A_WIDTH] = HEAD_DIM ** -0.5
    s[:, COL_IQ:COL_IQ + IQ_WIDTH] = IDX_DIM ** -0.5
    s[:, COL_MQ:COL_MQ + MQ_WIDTH] = M_DK ** -0.5
    s[:, COL_SM + SM_IW:COL_SM + SM_IW + IDX_HEADS] = IDX_HEADS ** -0.5
    return s


def _rope_tables(positions):
    inv = ROPE_THETA ** (-np.arange(0, HEAD_DIM, 2, dtype=np.float64) / HEAD_DIM)
    ang = positions.astype(np.float64)[:, None] * inv[None, :]
    cos = np.cos(ang)
    sin = np.sin(ang)
    cos128 = np.tile(cos, (1, 4))
    sin128 = np.tile(np.concatenate([-sin, sin], axis=1), (1, 2))
    return cos128.astype(np.float32), sin128.astype(np.float32)


def _proj_kernel(x_ref, w_ref, b_ref, s_ref, cos_ref, sin_ref, o_ref, xb_ref, *, full_rope, half_rope):
    j = pl.program_id(1)

    @pl.when(j == 0)
    def _():
        xb_ref[...] = x_ref[...].astype(BF16)

    y = jnp.dot(xb_ref[...], w_ref[...], preferred_element_type=F32) + b_ref[...]
    tn = y.shape[1]

    def rope(v):
        lane = lax.broadcasted_iota(jnp.int32, v.shape, 1)
        first = (lane & 32) == 0
        rot = jnp.where(first, pltpu.roll(v, LANES - 32, 1), pltpu.roll(v, 32, 1))
        return v * cos_ref[...] + rot * sin_ref[...]

    is_full = functools.reduce(jnp.logical_or, [j == f for f in full_rope])
    is_half = j == half_rope

    @pl.when(is_full)
    def _():
        for c in range(tn // LANES):
            sl = slice(c * LANES, (c + 1) * LANES)
            o_ref[:, sl] = rope(y[:, sl]) * s_ref[:, sl]

    @pl.when(is_half)
    def _():
        o_ref[:, :LANES] = rope(y[:, :LANES]) * s_ref[:, :LANES]
        o_ref[:, LANES:] = y[:, LANES:] * s_ref[:, LANES:]

    @pl.when(jnp.logical_not(jnp.logical_or(is_full, is_half)))
    def _():
        o_ref[...] = y * s_ref[...]


def _proj(x, w, b, cos_t, sin_t, *, tm, row0, n_rows):
    d = x.shape[1]
    rb0 = row0 // tm
    period = cos_t.shape[0] // tm
    full = tuple(range(COL_AQ // PROJ_TN, (COL_AQ + A_WIDTH) // PROJ_TN)) + \
        tuple(range(COL_IQ // PROJ_TN, (COL_IQ + IQ_WIDTH) // PROJ_TN))
    half = COL_SM // PROJ_TN

    def tab_map(i, j):
        return (i % period, 0)

    return pl.pallas_call(
        functools.partial(_proj_kernel, full_rope=full, half_rope=half),
        grid=(n_rows // tm, PROJ_COLS // PROJ_TN),
        in_specs=[
            pl.BlockSpec((tm, d), lambda i, j: (rb0 + i, 0)),
            pl.BlockSpec((d, PROJ_TN), lambda i, j: (0, j)),
            pl.BlockSpec((1, PROJ_TN), lambda i, j: (0, j)),
            pl.BlockSpec((1, PROJ_TN), lambda i, j: (0, j)),
            pl.BlockSpec((tm, LANES), tab_map),
            pl.BlockSpec((tm, LANES), tab_map),
        ],
        out_specs=pl.BlockSpec((tm, PROJ_TN), lambda i, j: (i, j)),
        out_shape=jax.ShapeDtypeStruct((n_rows, PROJ_COLS), F32),
        scratch_shapes=[pltpu.VMEM((tm, d), BF16)],
        compiler_params=_cparams(("parallel", "arbitrary")),
        name="in_proj",
    )(x, w, b, jnp.asarray(_col_scale()), cos_t, sin_t)


def _dsa_kernel(q_ref, iq_ref, sm_ref, k_ref, ki_ref, v_ref, o_ref,
                q2_ref, iq2_ref, key_ref, mrun_ref, mb_ref, acc_ref, *,
                tq, tk, n_keys, q_pos0, topk, causal, hpt):
    i = pl.program_id(1)
    nkb_all = key_ref.shape[0]
    nkb = (i + 1) if causal else nkb_all
    qbase = q_pos0 + i * tq
    rt = hpt * tq
    ntile = A_HEADS // hpt
    rep = tk // LANES

    def kb_loop(body, init):
        if causal:
            return lax.fori_loop(0, nkb, body, init)
        return lax.fori_loop(0, nkb_all, body, init, unroll=True)

    def wide(a):
        return jnp.concatenate([a] * rep, axis=1)

    for h in range(A_HEADS):
        q2_ref[h * tq:(h + 1) * tq, :] = q_ref[:, h * HEAD_DIM:(h + 1) * HEAD_DIM].astype(BF16)
    for h in range(IDX_HEADS):
        iq2_ref[h * tq:(h + 1) * tq, :] = iq_ref[:, h * IDX_DIM:(h + 1) * IDX_DIM].astype(BF16)
    w = sm_ref[:, SM_IW:SM_IW + IDX_HEADS]

    qpos = qbase + lax.broadcasted_iota(jnp.int32, (tq, 1), 0)
    nt = (((1,), (1,)), ((), ()))

    def score_block(kb, carry):
        ks = pl.multiple_of(kb * tk, tk)
        kib = ki_ref[pl.ds(ks, tk), :].astype(BF16)
        d = lax.dot_general(iq2_ref[...], kib, nt, preferred_element_type=F32)
        sc = jnp.zeros((tq, tk), F32)
        for h in range(IDX_HEADS):
            sc = sc + w[:, h:h + 1] * jnp.maximum(d[h * tq:(h + 1) * tq, :], 0.0)
        kpos = ks + lax.broadcasted_iota(jnp.int32, (1, tk), 1)
        adm = jnp.logical_and((kpos >> 6) <= (qpos >> 6), kpos < n_keys)
        bits = lax.bitcast_convert_type(sc, jnp.int32)
        key = bits ^ ((bits >> 31) & 0x7FFFFFFF)
        key_ref[kb] = jnp.where(adm, key, INT_MIN)
        return carry

    kb_loop(score_block, 0)
    if causal:
        def fill_block(kb, carry):
            key_ref[kb] = jnp.full((tq, tk), INT_MIN, jnp.int32)
            return carry
        lax.fori_loop(nkb, nkb_all, fill_block, 0)

    ones = jnp.ones((tk, LANES), BF16)

    def count(pred):
        acc = jnp.zeros((tq, LANES), F32)
        for kb in range(nkb_all):
            hit = jnp.where(pred(key_ref[kb]), 1.0, 0.0).astype(BF16)
            acc = acc + jnp.dot(hit, ones, preferred_element_type=F32)
        return acc

    kf = float(topk)

    def search(it, r):
        cand_b = r | (jnp.int32(1) << (31 - it))
        cand = wide(cand_b ^ INT_MIN)
        cnt = count(lambda kk: kk >= cand)
        return jnp.where(cnt >= kf, cand_b, r)

    r = lax.fori_loop(0, 32, search, jnp.zeros((tq, LANES), jnp.int32))
    thr = wide(r ^ INT_MIN)
    need = wide(kf - count(lambda kk: kk > thr))

    rowi = lax.broadcasted_iota(jnp.int32, (tk, tk), 0)
    coli = lax.broadcasted_iota(jnp.int32, (tk, tk), 1)
    before = jnp.where(rowi < coli, 1.0, 0.0).astype(BF16)

    def select_block(kb, off):
        kk = key_ref[kb]
        eq = kk == thr
        eqb = jnp.where(eq, 1.0, 0.0).astype(BF16)
        rank = jnp.dot(eqb, before, preferred_element_type=F32) + wide(off)
        sel = jnp.logical_or(kk > thr, jnp.logical_and(eq, rank < need))
        sel = jnp.logical_and(sel, kk != INT_MIN)
        key_ref[kb] = jnp.where(sel, 1, 0).astype(jnp.int32)
        return off + jnp.dot(eqb, ones, preferred_element_type=F32)

    kb_loop(select_block, jnp.zeros((tq, LANES), F32))

    def tile_mask(kb):
        m = jnp.where(key_ref[kb] != 0, 1.0, 0.0)
        return m if hpt == 1 else jnp.concatenate([m] * hpt, axis=0)

    def logits(j, kblk):
        return lax.dot_general(q2_ref[j * rt:(j + 1) * rt, :], kblk, nt, preferred_element_type=F32)

    mrun_ref[...] = jnp.full(mrun_ref.shape, NEG_BIG, F32)

    def max_block(kb, carry):
        ks = pl.multiple_of(kb * tk, tk)
        kblk = k_ref[pl.ds(ks, tk), :].astype(BF16)
        selt = tile_mask(kb) > 0.0
        for j in range(ntile):
            mrun_ref[j] = jnp.maximum(mrun_ref[j], jnp.where(selt, logits(j, kblk), NEG_BIG))
        return carry

    kb_loop(max_block, 0)
    for j in range(ntile):
        mb_ref[j] = jnp.broadcast_to(jnp.max(mrun_ref[j], axis=1, keepdims=True), (rt, LANES))

    acc_ref[...] = jnp.zeros(acc_ref.shape, F32)

    def pv_block(kb, carry):
        ks = pl.multiple_of(kb * tk, tk)
        kblk = k_ref[pl.ds(ks, tk), :].astype(BF16)
        vblk = v_ref[pl.ds(ks, tk), :].astype(BF16)
        selt = tile_mask(kb) > 0.0
        for j in range(ntile):
            p = jnp.where(selt, jnp.exp(logits(j, kblk) - wide(mb_ref[j])), 0.0)
            acc_ref[j] += jnp.dot(p.astype(BF16), vblk, preferred_element_type=F32)
        return carry

    kb_loop(pv_block, 0)

    for j in range(ntile):
        a = acc_ref[j]
        o = a[:, :HEAD_DIM] / a[:, HEAD_DIM:HEAD_DIM + 1]
        for hh in range(hpt):
            h = j * hpt + hh
            o_ref[:, h * HEAD_DIM:(h + 1) * HEAD_DIM] = o[hh * tq:(hh + 1) * tq, :].astype(o_ref.dtype)


def _dsa(proj, nb, t, k, ki, v_ext, *, q_pos0, n_keys, tq, topk, causal):
    tk = 256
    s_pad = k.shape[1]
    nq = t // tq
    if causal:
        assert tq == tk and tq % CHUNK == 0
    rows_tile = 256
    hpt = max(1, rows_tile // tq)
    kern = functools.partial(_dsa_kernel, tq=tq, tk=tk, n_keys=n_keys, q_pos0=q_pos0,
                             topk=topk, causal=causal, hpt=hpt)
    row_map = lambda c: (lambda b, i: (b * nq + i, c))
    kv_spec = pl.BlockSpec((None, s_pad, HEAD_DIM), lambda b, i: (b, 0, 0))
    ntile = A_HEADS // hpt
    rt = hpt * tq
    return pl.pallas_call(
        kern,
        grid=(nb, nq),
        in_specs=[
            pl.BlockSpec((tq, A_WIDTH), row_map(COL_AQ // A_WIDTH)),
            pl.BlockSpec((tq, IQ_WIDTH), row_map(COL_IQ // IQ_WIDTH)),
            pl.BlockSpec((tq, SM_WIDTH), row_map(COL_SM // SM_WIDTH)),
            kv_spec, kv_spec,
            pl.BlockSpec((None, s_pad, LANES), lambda b, i: (b, 0, 0)),
        ],
        out_specs=pl.BlockSpec((tq, A_WIDTH), lambda b, i: (b * nq + i, 0)),
        out_shape=jax.ShapeDtypeStruct((nb * t, A_WIDTH), BF16),
        scratch_shapes=[
            pltpu.VMEM((A_HEADS * tq, HEAD_DIM), BF16),
            pltpu.VMEM((IDX_HEADS * tq, IDX_DIM), BF16),
            pltpu.VMEM((s_pad // tk, tq, tk), jnp.int32),
            pltpu.VMEM((ntile, rt, tk), F32),
            pltpu.VMEM((ntile, rt, LANES), F32),
            pltpu.VMEM((ntile, rt, LANES), F32),
        ],
        compiler_params=_cparams(("parallel", "arbitrary")),
        name="dsa_causal" if causal else "dsa_cached",
    )(proj, proj, proj, k, ki, v_ext)


def _log_sigmoid(x):
    return jnp.minimum(x, 0.0) - jnp.log(1.0 + jnp.exp(-jnp.abs(x)))


def _mlstm_kernel(q_ref, k_ref, v_ref, og_ref, sm_ref, gt_ref, c0_ref, n0_ref, m0_ref, gn_ref,
                  h_ref, c_out, n_out, m_out, ct_s, n_s, m_s, *, chunk, n_chunks):
    c = pl.program_id(1)
    L = chunk

    @pl.when(c == 0)
    def _():
        for h in range(M_HEADS):
            ct_s[h] = c0_ref[h].T
        n_s[...] = n0_ref[...]
        m_s[...] = m0_ref[...]

    ri = lax.broadcasted_iota(jnp.int32, (L, L), 0)
    ci = lax.broadcasted_iota(jnp.int32, (L, L), 1)
    tril = ci <= ri
    lower = jnp.where(tril, 1.0, 0.0)
    upper = jnp.where(ri <= ci, 1.0, 0.0)
    hi = lax.Precision.HIGHEST
    nt = (((1,), (1,)), ((), ()))

    for ch in range(n_chunks):
        rows = slice(ch * L, (ch + 1) * L)
        i_col = sm_ref[rows, SM_MI:SM_MI + M_HEADS]
        f_col = _log_sigmoid(sm_ref[rows, SM_MF:SM_MF + M_HEADS])
        i_row = gt_ref[0:M_HEADS, rows]
        f_row = _log_sigmoid(gt_ref[M_HEADS:2 * M_HEADS, rows])
        b_col = jnp.dot(lower, f_col, precision=hi, preferred_element_type=F32)
        b_row = jnp.dot(f_row, upper, precision=hi, preferred_element_type=F32)
        for h in range(M_HEADS):
            bc = b_col[:, h:h + 1]
            br = b_row[h:h + 1, :]
            ir = i_row[h:h + 1, :]
            ic = i_col[:, h:h + 1]
            m = m_s[h:h + 1, 0:1]
            a = bc + m
            dm = jnp.where(tril, bc - br + ir, -jnp.inf)
            mt = jnp.maximum(a, jnp.max(dm, axis=1, keepdims=True))
            inter = jnp.exp(a - mt)
            qh = q_ref[rows, h * M_DK:(h + 1) * M_DK]
            kh = k_ref[rows, h * M_DK:(h + 1) * M_DK]
            vh = v_ref[rows, h * M_DV:(h + 1) * M_DV]
            qb = qh.astype(BF16)
            kb = kh.astype(BF16)
            s = lax.dot_general(qb, kb, nt, preferred_element_type=F32) * jnp.exp(dm - mt)
            ct = ct_s[h]
            num = jnp.dot(s.astype(BF16), vh.astype(BF16), preferred_element_type=F32) + \
                inter * jnp.dot(qb, ct.astype(BF16), preferred_element_type=F32)
            nrow = n_s[h:h + 1, :]
            den = jnp.sum(s, axis=1, keepdims=True) + inter * jnp.sum(qh * nrow, axis=1, keepdims=True)
            hc = num / jnp.maximum(jnp.abs(den), jnp.exp(-mt))
            bl = bc[L - 1:L, :]
            g_c = bl - bc + ic
            g_r = bl - br + ir
            m_new = jnp.maximum(bl + m, jnp.max(g_r, axis=1, keepdims=True))
            decay = jnp.exp(bl + m - m_new)
            ws = jnp.exp(g_c - m_new)
            kt = kh.T.astype(BF16)
            ct_s[h] = decay * ct + jnp.dot(kt, (ws * vh).astype(BF16), preferred_element_type=F32)
            n_s[h:h + 1, :] = decay * nrow + jnp.sum(ws * kh, axis=0, keepdims=True)
            m_s[h:h + 1, :] = jnp.broadcast_to(m_new, (1, LANES))
            mu = jnp.mean(hc, axis=1, keepdims=True)
            hz = hc - mu
            var = jnp.mean(hz * hz, axis=1, keepdims=True)
            cols = slice(h * M_DV, (h + 1) * M_DV)
            hn = hz * lax.rsqrt(var + LN_EPS) * gn_ref[:, cols]
            h_ref[rows, cols] = (hn * jax.nn.sigmoid(og_ref[rows, cols])).astype(h_ref.dtype)

    @pl.when(c == pl.num_programs(1) - 1)
    def _():
        for h in range(M_HEADS):
            c_out[h] = ct_s[h].T
        n_out[...] = n_s[...]
        m_out[...] = m_s[...]


def _mlstm(proj, nb, t, gates_t, c0, n0, m0b, gnorm):
    chunk = CHUNK if t % CHUNK == 0 else t
    tc = min(t, 256)
    nct = t // tc
    row_map = lambda col: (lambda b, c: (b * nct + c, col))
    kern = functools.partial(_mlstm_kernel, chunk=chunk, n_chunks=tc // chunk)
    return pl.pallas_call(
        kern,
        grid=(nb, nct),
        in_specs=[
            pl.BlockSpec((tc, MQ_WIDTH), row_map(COL_MQ // MQ_WIDTH)),
            pl.BlockSpec((tc, MQ_WIDTH), row_map(COL_MK // MQ_WIDTH)),
            pl.BlockSpec((tc, M_WIDTH), row_map(COL_MV // M_WIDTH)),
            pl.BlockSpec((tc, M_WIDTH), row_map(COL_MO // M_WIDTH)),
            pl.BlockSpec((tc, SM_WIDTH), row_map(COL_SM // SM_WIDTH)),
            pl.BlockSpec((None, 2 * M_HEADS, tc), lambda b, c: (b, 0, c)),
            pl.BlockSpec((None, M_HEADS, M_DV, M_DK), lambda b, c: (b, 0, 0, 0)),
            pl.BlockSpec((None, M_HEADS, M_DK), lambda b, c: (b, 0, 0)),
            pl.BlockSpec((None, SUBLANES, LANES), lambda b, c: (b, 0, 0)),
            pl.BlockSpec((1, M_WIDTH), lambda b, c: (0, 0)),
        ],
        out_specs=[
            pl.BlockSpec((tc, M_WIDTH), lambda b, c: (b * nct + c, 0)),
            pl.BlockSpec((None, M_HEADS, M_DV, M_DK), lambda b, c: (b, 0, 0, 0)),
            pl.BlockSpec((None, M_HEADS, M_DK), lambda b, c: (b, 0, 0)),
            pl.BlockSpec((None, SUBLANES, LANES), lambda b, c: (b, 0, 0)),
        ],
        out_shape=[
            jax.ShapeDtypeStruct((nb * t, M_WIDTH), BF16),
            jax.ShapeDtypeStruct((nb, M_HEADS, M_DV, M_DK), F32),
            jax.ShapeDtypeStruct((nb, M_HEADS, M_DK), F32),
            jax.ShapeDtypeStruct((nb, SUBLANES, LANES), F32),
        ],
        scratch_shapes=[
            pltpu.VMEM((M_HEADS, M_DK, M_DV), F32),
            pltpu.VMEM((M_HEADS, M_DK), F32),
            pltpu.VMEM((SUBLANES, LANES), F32),
        ],
        compiler_params=_cparams(("parallel", "arbitrary")),
        name="mlstm_chunked" if nct > 1 else "mlstm_single",
    )(proj, proj, proj, proj, proj, gates_t, c0, n0, m0b, gnorm)


def _layer_norm(z, g, b):
    mu = jnp.mean(z, axis=1, keepdims=True)
    zc = z - mu
    var = jnp.mean(zc * zc, axis=1, keepdims=True)
    return zc * lax.rsqrt(var + LN_EPS) * g + b


def _mix_kernel(a_ref, h_ref, x_ref, wa_ref, wh_ref, g_ref, b_ref, rw_ref, rb_ref,
                x1_ref, x1t_ref, ti_ref, tg_ref, *, alpha, rpt):
    y = jnp.dot(a_ref[...], wa_ref[...], preferred_element_type=F32) + \
        jnp.dot(h_ref[...], wh_ref[...], preferred_element_type=F32)
    x1 = _layer_norm(alpha * x_ref[...] + y, g_ref[...], b_ref[...])
    x1_ref[...] = x1
    tm = x1.shape[0]
    for c in range(rpt):
        x1t_ref[pl.ds(c, tm, stride=rpt), :] = x1[:, c * LANES:(c + 1) * LANES]
    logits = jnp.dot(x1.astype(BF16), rw_ref[...], preferred_element_type=F32) + rb_ref[...]
    lane = lax.broadcasted_iota(jnp.int32, logits.shape, 1)
    lanef = lane.astype(F32)
    cur = jnp.where(lane < N_EXPERTS, logits, -jnp.inf)
    vals, idxs = [], []
    for _ in range(TOP_K):
        mx = jnp.max(cur, axis=1, keepdims=True)
        ix = jnp.min(jnp.where(cur == mx, lanef, float(LANES)), axis=1, keepdims=True)
        vals.append(mx)
        idxs.append(ix)
        cur = jnp.where(lanef == ix, -jnp.inf, cur)
    es = [jnp.exp(v - vals[0]) for v in vals]
    tot = functools.reduce(lambda p, q: p + q, es)
    ti = jnp.zeros(logits.shape, F32)
    tg = jnp.zeros(logits.shape, F32)
    for j in range(TOP_K):
        ti = jnp.where(lane == j, idxs[j], ti)
        tg = jnp.where(lane == j, es[j] / tot, tg)
    ti_ref[...] = ti[:, :SUBLANES].astype(jnp.int32)
    tg_ref[...] = tg[:, :SUBLANES]


def _mix(attn, hm, x, wa, wh, g, b, rw, rb, *, alpha, tm):
    n, d = x.shape
    rpt = d // LANES
    kern = functools.partial(_mix_kernel, alpha=alpha, rpt=rpt)
    const = lambda shape: pl.BlockSpec(shape, lambda i: (0,) * len(shape))
    return pl.pallas_call(
        kern,
        grid=(n // tm,),
        in_specs=[
            pl.BlockSpec((tm, A_WIDTH), lambda i: (i, 0)),
            pl.BlockSpec((tm, M_WIDTH), lambda i: (i, 0)),
            pl.BlockSpec((tm, d), lambda i: (i, 0)),
            const((A_WIDTH, d)), const((M_WIDTH, d)),
            const((1, d)), const((1, d)),
            const((d, LANES)), const((1, LANES)),
        ],
        out_specs=[
            pl.BlockSpec((tm, d), lambda i: (i, 0)),
            pl.BlockSpec((tm * rpt, LANES), lambda i: (i, 0)),
            pl.BlockSpec((tm, SUBLANES), lambda i: (i, 0)),
            pl.BlockSpec((tm, SUBLANES), lambda i: (i, 0)),
        ],
        out_shape=[
            jax.ShapeDtypeStruct((n, d), F32),
            jax.ShapeDtypeStruct((n * rpt, LANES), F32),
            jax.ShapeDtypeStruct((n, SUBLANES), jnp.int32),
            jax.ShapeDtypeStruct((n, SUBLANES), F32),
        ],
        compiler_params=_cparams(("parallel",)),
        name="mix_ln_router",
    )(attn, hm, x, wa, wh, g, b, rw, rb)


def _gather_pipeline(blk, n_active, idx_hbm, src_hbm, idx_s, buf, sem_i, sem_r, *, n_rows, rpt):
    slot = blk & 1
    nslot = 1 - slot

    def idx_copy(b, s):
        return pltpu.make_async_copy(idx_hbm.at[pl.ds(pl.multiple_of(b * SUBLANES, SUBLANES), SUBLANES), :],
                                     idx_s.at[s], sem_i.at[s])

    def issue_rows(s):
        def body(r, c):
            src_row = idx_s[s, lax.shift_right_logical(r, LANE_BITS), r & (LANES - 1)]
            pltpu.make_async_copy(
                src_hbm.at[pl.ds(pl.multiple_of(src_row * rpt, rpt), rpt), :],
                buf.at[s, pl.ds(pl.multiple_of(r * rpt, rpt), rpt), :],
                sem_r.at[s]).start()
            return c
        lax.fori_loop(0, n_rows, body, 0, unroll=8)

    @pl.when(blk == 0)
    def _():
        first = idx_copy(0, 0)
        first.start()
        first.wait()
        issue_rows(0)

        @pl.when(n_active > 1)
        def _():
            idx_copy(1, 1).start()

    @pl.when(blk + 1 < n_active)
    def _():
        idx_copy(blk + 1, nslot).wait()
        issue_rows(nslot)

        @pl.when(blk + 2 < n_active)
        def _():
            idx_copy(blk + 2, slot).start()

    pltpu.make_async_copy(buf.at[slot], buf.at[slot], sem_r.at[slot]).wait()
    return slot


def _moe_kernel(be_ref, nu_ref, idx_hbm, x_hbm, wg_ref, bg_ref, wu_ref, bu_ref, wd_ref, bd_ref,
                y_ref, idx_s, xbuf, xb, sem_i, sem_r, *, tm, rpt):
    blk = pl.program_id(0)
    n_active = nu_ref[0]

    @pl.when(blk < n_active)
    def _():
        slot = _gather_pipeline(blk, n_active, idx_hbm, x_hbm, idx_s, xbuf, sem_i, sem_r,
                                n_rows=tm, rpt=rpt)
        for c in range(rpt):
            xb[:, c * LANES:(c + 1) * LANES] = xbuf[slot, pl.ds(c, tm, stride=rpt), :].astype(BF16)
        xv = xb[...]
        g = jnp.minimum(jnp.dot(xv, wg_ref[...], preferred_element_type=F32) + bg_ref[...], SWIGLU_LIMIT)
        u = jnp.clip(jnp.dot(xv, wu_ref[...], preferred_element_type=F32) + bu_ref[...],
                     -SWIGLU_LIMIT, SWIGLU_LIMIT)
        hdn = g * jax.nn.sigmoid(SWIGLU_ALPHA * g) * (u + 1.0)
        y = jnp.dot(hdn.astype(BF16), wd_ref[...], preferred_element_type=F32) + bd_ref[...]
        for c in range(rpt):
            y_ref[pl.ds(c, tm, stride=rpt), :] = y[:, c * LANES:(c + 1) * LANES]

    @pl.when(blk >= n_active)
    def _():
        y_ref[...] = jnp.zeros(y_ref.shape, y_ref.dtype)


def _moe_experts(block_expert, n_active, idx_tiles, x1t, wg, bg, wu, bu, wd, bd, *, tm, d):
    n_blocks = block_expert.shape[0]
    rpt = d // LANES
    de = wg.shape[-1]
    kern = functools.partial(_moe_kernel, tm=tm, rpt=rpt)
    wmap = lambda i, be, nu: (be[i], 0, 0)
    grid_spec = pltpu.PrefetchScalarGridSpec(
        num_scalar_prefetch=2,
        grid=(n_blocks,),
        in_specs=[
            pl.BlockSpec(memory_space=pl.ANY),
            pl.BlockSpec(memory_space=pl.ANY),
            pl.BlockSpec((None, d, de), wmap), pl.BlockSpec((None, 1, de), wmap),
            pl.BlockSpec((None, d, de), wmap), pl.BlockSpec((None, 1, de), wmap),
            pl.BlockSpec((None, de, d), wmap), pl.BlockSpec((None, 1, d), wmap),
        ],
        out_specs=pl.BlockSpec((tm * rpt, LANES), lambda i, be, nu: (i, 0)),
        scratch_shapes=[
            pltpu.SMEM((2, SUBLANES, LANES), jnp.int32),
            pltpu.VMEM((2, tm * rpt, LANES), F32),
            pltpu.VMEM((tm, d), BF16),
            pltpu.SemaphoreType.DMA((2,)),
            pltpu.SemaphoreType.DMA((2,)),
        ],
    )
    return pl.pallas_call(
        kern,
        grid_spec=grid_spec,
        out_shape=jax.ShapeDtypeStruct((n_blocks * tm * rpt, LANES), F32),
        compiler_params=_cparams(("arbitrary",)),
        name="moe_experts",
    )(block_expert, n_active, idx_tiles, x1t, wg, bg, wu, bu, wd, bd)


def _combine_kernel(idx_hbm, y_hbm, x_ref, tg_ref, g_ref, b_ref, o_ref, idx_s, ybuf, sem_i, sem_r,
                    *, tm, rpt, alpha):
    blk = pl.program_id(0)
    slot = _gather_pipeline(blk, pl.num_programs(0), idx_hbm, y_hbm, idx_s, ybuf, sem_i, sem_r,
                            n_rows=TOP_K * tm, rpt=rpt)
    gates = tg_ref[...]
    cols = []
    for c in range(rpt):
        acc = alpha * x_ref[:, c * LANES:(c + 1) * LANES]
        for j in range(TOP_K):
            acc = acc + gates[:, j:j + 1] * ybuf[slot, pl.ds(j * tm * rpt + c, tm, stride=rpt), :]
        cols.append(acc)
    z = jnp.concatenate(cols, axis=1)
    o_ref[...] = _layer_norm(z, g_ref[...], b_ref[...])


def _combine(idx_tiles, yb, x1, tg, g, b, *, alpha, tm):
    n, d = x1.shape
    rpt = d // LANES
    kern = functools.partial(_combine_kernel, tm=tm, rpt=rpt, alpha=alpha)
    return pl.pallas_call(
        kern,
        grid=(n // tm,),
        in_specs=[
            pl.BlockSpec(memory_space=pl.ANY),
            pl.BlockSpec(memory_space=pl.ANY),
            pl.BlockSpec((tm, d), lambda i: (i, 0)),
            pl.BlockSpec((tm, SUBLANES), lambda i: (i, 0)),
            pl.BlockSpec((1, d), lambda i: (0, 0)),
            pl.BlockSpec((1, d), lambda i: (0, 0)),
        ],
        out_specs=pl.BlockSpec((tm, d), lambda i: (i, 0)),
        out_shape=jax.ShapeDtypeStruct((n, d), F32),
        scratch_shapes=[
            pltpu.SMEM((2, SUBLANES, LANES), jnp.int32),
            pltpu.VMEM((2, TOP_K * tm * rpt, LANES), F32),
            pltpu.SemaphoreType.DMA((2,)),
            pltpu.SemaphoreType.DMA((2,)),
        ],
        compiler_params=_cparams(("arbitrary",)),
        name="moe_combine_ln",
    )(idx_tiles, yb, x1, tg, g, b)


def _routing(top_idx, *, tm_e, tm_c):
    n = top_idx.shape[0]
    a = n * TOP_K
    n_blocks = (a + N_EXPERTS * (tm_e - 1) + tm_e - 1) // tm_e
    r = n_blocks * tm_e
    e_flat = top_idx.reshape(-1)
    order = jnp.argsort(e_flat, stable=True).astype(jnp.int32)
    inv = jnp.argsort(order).astype(jnp.int32)
    experts = jnp.arange(N_EXPERTS, dtype=jnp.int32)
    counts = jnp.sum((e_flat[:, None] == experts[None, :]).astype(jnp.int32), axis=0)
    padded = ((counts + tm_e - 1) // tm_e) * tm_e
    pad_end = jnp.cumsum(padded)
    pad_start = pad_end - padded
    grp_start = jnp.cumsum(counts) - counts
    n_active = (pad_end[-1] // tm_e).astype(jnp.int32).reshape(1)
    blk_start = jnp.minimum(jnp.arange(n_blocks, dtype=jnp.int32), n_active[0] - 1) * tm_e
    block_expert = jnp.minimum(jnp.sum((pad_end[None, :] <= blk_start[:, None]).astype(jnp.int32), axis=1),
                               N_EXPERTS - 1)
    e_row = jnp.repeat(block_expert, tm_e)
    off = jnp.arange(r, dtype=jnp.int32) - pad_start[e_row]
    src = jnp.clip(grp_start[e_row] + off, 0, a - 1)
    rows_tok = jnp.where(off < counts[e_row], order[src] // TOP_K, 0).astype(jnp.int32)
    pos = (pad_start[e_flat] + inv - grp_start[e_flat]).astype(jnp.int32).reshape(n, TOP_K)

    def tiles(idx2d):
        nb, per = idx2d.shape
        padw = SUBLANES * LANES - per
        return jnp.pad(idx2d, ((0, 0), (0, padw))).reshape(nb * SUBLANES, LANES)

    idx_e = tiles(rows_tok.reshape(n_blocks, tm_e))
    idx_c = tiles(pos.reshape(n // tm_c, tm_c, TOP_K).transpose(0, 2, 1).reshape(n // tm_c, TOP_K * tm_c))
    return block_expert, n_active, idx_e, idx_c


def kernel(x_prompt, x_sample, cache_k, cache_v, cache_kidx, state_C, state_n, state_m,
           w_in, b_in, w_out, mh_norm_g, ln1_g, ln1_b, router_w, router_b,
           w_gate, b_gate, w_up, b_up, w_down, b_down, ln2_g, ln2_b):
    depth = w_in.shape[0]
    bp, tp, d = x_prompt.shape
    bs, ts, _ = x_sample.shape
    past = cache_k.shape[2]
    n_p, n_s = bp * tp, bs * ts
    n = n_p + n_s
    alpha = float((2.0 * depth) ** 0.25)
    tm_p, tm = 1024, 512
    tm_e, tm_c = 256, 128
    tq_p = 256
    assert tp % tm_p == 0 and n_p % tm == 0 and n_s % tm == 0 and n % tm_c == 0 and tp % tq_p == 0
    topk_p = min(TOPK_MAX, tp // 4)
    topk_s = min(TOPK_MAX, (past + ts) // 4)
    s_keys = past + ts
    s_pad = -(-s_keys // 256) * 256

    cos_p, sin_p = (jnp.asarray(t) for t in _rope_tables(np.arange(tp)))
    cos_s, sin_s = (jnp.asarray(t) for t in _rope_tables(np.tile(past + np.arange(ts), bs)))

    x = jnp.concatenate([x_prompt.reshape(n_p, d), x_sample.reshape(n_s, d)], axis=0)
    w_in_k = _arrange_cols(w_in).astype(BF16)
    b_in_k = _arrange_cols(b_in).reshape(depth, 1, PROJ_COLS)
    w_out_k = w_out.astype(BF16)
    rw_k = jnp.pad(router_w, ((0, 0), (0, 0), (0, LANES - N_EXPERTS))).astype(BF16)
    rb_k = jnp.pad(router_b, ((0, 0), (0, LANES - N_EXPERTS))).reshape(depth, 1, LANES)
    wg_k, wu_k, wd_k = w_gate.astype(BF16), w_up.astype(BF16), w_down.astype(BF16)
    de = w_gate.shape[-1]

    zeros_c = jnp.zeros((bp, M_HEADS, M_DV, M_DK), F32)
    zeros_n = jnp.zeros((bp, M_HEADS, M_DK), F32)
    zeros_m = jnp.zeros((bp, SUBLANES, LANES), F32)

    def split_small(proj, nb, t):
        small = proj[:, COL_SM:COL_SM + SM_WIDTH]
        k = small[:, SM_AK:SM_AK + HEAD_DIM].reshape(nb, t, HEAD_DIM)
        ki = small[:, SM_IK:SM_IK + IDX_DIM].reshape(nb, t, IDX_DIM)
        v = small[:, SM_AV:SM_AV + HEAD_DIM].reshape(nb, t, HEAD_DIM)
        gates_t = small[:, SM_MI:SM_MI + 2 * M_HEADS].reshape(nb, t, 2 * M_HEADS).transpose(0, 2, 1)
        return k, ki, v, gates_t

    def with_ones(v):
        pad = jnp.zeros(v.shape[:-1] + (LANES - HEAD_DIM - 1,), v.dtype)
        return jnp.concatenate([v, jnp.ones(v.shape[:-1] + (1,), v.dtype), pad], axis=-1)

    def pad_keys(a):
        return jnp.pad(a, ((0, 0), (0, s_pad - s_keys), (0, 0)))

    outs = {name: [] for name in ("kp", "vp", "ip", "cp", "np", "mp", "ks", "vs", "is", "cs", "ns", "ms")}
    for l in range(depth):
        proj_p = _proj(x, w_in_k[l], b_in_k[l], cos_p, sin_p, tm=tm_p, row0=0, n_rows=n_p)
        proj_s = _proj(x, w_in_k[l], b_in_k[l], cos_s, sin_s, tm=tm, row0=n_p, n_rows=n_s)
        kp, ip, vp, gt_p = split_small(proj_p, bp, tp)
        ks, is_, vs, gt_s = split_small(proj_s, bs, ts)

        attn_p = _dsa(proj_p, bp, tp, kp, ip, with_ones(vp), q_pos0=0, n_keys=tp, tq=tq_p,
                      topk=topk_p, causal=True)
        attn_s = _dsa(proj_s, bs, ts,
                      pad_keys(jnp.concatenate([cache_k[l], ks], axis=1)),
                      pad_keys(jnp.concatenate([cache_kidx[l], is_], axis=1)),
                      pad_keys(with_ones(jnp.concatenate([cache_v[l], vs], axis=1))),
                      q_pos0=past, n_keys=s_keys, tq=ts, topk=topk_s, causal=False)
        gn = mh_norm_g[l].reshape(1, M_WIDTH)
        hm_p, c_p, nn_p, m_p = _mlstm(proj_p, bp, tp, gt_p, zeros_c, zeros_n, zeros_m, gn)
        m0s = jnp.broadcast_to(jnp.pad(state_m[l], ((0, 0), (0, SUBLANES - M_HEADS)))[:, :, None],
                               (bs, SUBLANES, LANES))
        hm_s, c_s, nn_s, m_s = _mlstm(proj_s, bs, ts, gt_s, state_C[l], state_n[l], m0s, gn)

        attn = jnp.concatenate([attn_p, attn_s], axis=0)
        hm = jnp.concatenate([hm_p, hm_s], axis=0)
        x1, x1t, top_i, top_g = _mix(attn, hm, x, w_out_k[l, :A_WIDTH], w_out_k[l, A_WIDTH:],
                                     ln1_g[l].reshape(1, d), ln1_b[l].reshape(1, d), rw_k[l], rb_k[l],
                                     alpha=alpha, tm=tm)
        block_expert, n_active, idx_e, idx_c = _routing(top_i[:, :TOP_K], tm_e=tm_e, tm_c=tm_c)
        yb = _moe_experts(block_expert, n_active, idx_e, x1t,
                          wg_k[l], b_gate[l].reshape(N_EXPERTS, 1, de),
                          wu_k[l], b_up[l].reshape(N_EXPERTS, 1, de),
                          wd_k[l], b_down[l].reshape(N_EXPERTS, 1, d), tm=tm_e, d=d)
        x = _combine(idx_c, yb, x1, top_g, ln2_g[l].reshape(1, d), ln2_b[l].reshape(1, d),
                     alpha=alpha, tm=tm_c)

        for key, val in (("kp", kp), ("vp", vp), ("ip", ip), ("cp", c_p), ("np", nn_p),
                         ("mp", m_p[:, :M_HEADS, 0]), ("ks", ks), ("vs", vs), ("is", is_),
                         ("cs", c_s), ("ns", nn_s), ("ms", m_s[:, :M_HEADS, 0])):
            outs[key].append(val)

    st = {key: jnp.stack(val) for key, val in outs.items()}
    y_p = x[:n_p].reshape(bp, tp, d)
    y_s = x[n_p:].reshape(bs, ts, d)
    return (y_p, y_s, st["kp"], st["vp"], st["ip"], st["cp"], st["np"], st["mp"],
            st["ks"], st["vs"], st["is"], st["cs"], st["ns"], st["ms"])
```
